```python
import math
import jax, jax.numpy as jnp
from jax import lax
import numpy as np

D_MODEL = 1024
BATCH = 32
SEQ = 2048
DEPTH = 1

C_CONV = 1024
CONV_WIDTH = 31
HEAD_DIM = 64
N_Q_HEADS = 16
N_KV_HEADS = 2
GROUP = N_Q_HEADS // N_KV_HEADS
WINDOW = 128
BLOCK = 128
D_FF = int(math.ceil((8 * D_MODEL / 3) / 256) * 256)
EPS = 1e-6
NEG = -1e30

Q_W = N_Q_HEADS * HEAD_DIM
KV_W = N_KV_HEADS * HEAD_DIM
IN_COLS = 2 * C_CONV + Q_W + 2 * KV_W + 2 * D_MODEL

kernel_name = "hybrid_conformer_conv_swa_sink_alibi_block"


def rms_norm(x, g):
    xf = x.astype(jnp.float32)
    y = xf * lax.rsqrt(jnp.mean(xf * xf, axis=-1, keepdims=True) + EPS)
    return (y * g.astype(jnp.float32)).astype(x.dtype)


def layer_norm(x, g, b):
    xf = x.astype(jnp.float32)
    mu = jnp.mean(xf, axis=-1, keepdims=True)
    var = jnp.mean(jnp.square(xf - mu), axis=-1, keepdims=True)
    y = (xf - mu) * lax.rsqrt(var + EPS)
    return (y * g.astype(jnp.float32) + b.astype(jnp.float32)).astype(x.dtype)


def conv_module(u, dw_w, dw_b, ln_g, ln_b, w_out):
    a, gate = jnp.split(u, 2, axis=-1)
    h = a * jax.nn.sigmoid(gate)
    h = lax.conv_general_dilated(
        h, dw_w, window_strides=(1,), padding=[(CONV_WIDTH - 1, 0)],
        dimension_numbers=('NWC', 'WIO', 'NWC'),
        feature_group_count=C_CONV) + dw_b
    h = jax.nn.silu(layer_norm(h, ln_g, ln_b))
    return h @ w_out


def alibi_slopes(n_heads):
    h = jnp.arange(1, n_heads + 1, dtype=jnp.float32)
    return jnp.exp2(-8.0 * h / n_heads)


def swa_attention(q, k, v, q_g, k_g, sinks):
    B, S = q.shape[0], q.shape[1]
    nb = S // BLOCK
    q = rms_norm(q, q_g)
    k = rms_norm(k, k_g)
    qb = q.reshape(B, nb, BLOCK, N_KV_HEADS, GROUP, HEAD_DIM)
    pad = ((0, 0), (BLOCK, 0), (0, 0), (0, 0))
    kb = jnp.pad(k, pad).reshape(B, nb + 1, BLOCK, N_KV_HEADS, HEAD_DIM)
    vb = jnp.pad(v, pad).reshape(B, nb + 1, BLOCK, N_KV_HEADS, HEAD_DIM)
    kwin = jnp.concatenate([kb[:, :-1], kb[:, 1:]], axis=2)
    vwin = jnp.concatenate([vb[:, :-1], vb[:, 1:]], axis=2)

    scale = 1.0 / math.sqrt(HEAD_DIM)
    s = jnp.einsum('bnqkgd,bnskd->bnkgqs', qb, kwin).astype(jnp.float32) * scale

    qi = jnp.arange(BLOCK)[:, None]
    sj = jnp.arange(2 * BLOCK)[None, :]
    dist = qi + BLOCK - sj
    s_glob = jnp.arange(nb)[:, None, None] * BLOCK - BLOCK + sj
    valid = (dist >= 0) & (dist < WINDOW) & (s_glob >= 0)
    slopes = alibi_slopes(N_Q_HEADS).reshape(N_KV_HEADS, GROUP, 1, 1)
    bias = -slopes * dist.astype(jnp.float32)
    s = jnp.where(valid[None, :, None, None], s + bias, NEG)

    sink = sinks.astype(jnp.float32).reshape(N_KV_HEADS, GROUP, 1, 1)
    m = jnp.maximum(jnp.max(s, axis=-1, keepdims=True), sink)
    p = jnp.exp(s - m)
    p = p / (jnp.sum(p, axis=-1, keepdims=True) + jnp.exp(sink - m))
    o = jnp.einsum('bnkgqs,bnskd->bnqkgd', p.astype(v.dtype), vwin)
    return o.reshape(B, S, N_Q_HEADS * HEAD_DIM)


def setup_inputs(seed: int = 0) -> dict:
    key = jax.random.key(seed)
    ks = jax.random.split(key, 17)
    f32 = jnp.float32
    nrm = lambda k, shape, s: jax.random.normal(k, shape, f32) * s
    return {
        "x": nrm(ks[0], (BATCH, SEQ, D_MODEL), 1.0),
        "norm_mix_g": 1.0 + nrm(ks[1], (D_MODEL,), 0.02),
        "w_in": nrm(ks[2], (D_MODEL, IN_COLS), D_MODEL ** -0.5),
        "conv_dw_w": nrm(ks[3], (CONV_WIDTH, 1, C_CONV), CONV_WIDTH ** -0.5),
        "conv_dw_b": nrm(ks[4], (C_CONV,), 0.02),
        "conv_ln_g": 1.0 + nrm(ks[5], (C_CONV,), 0.02),
        "conv_ln_b": nrm(ks[6], (C_CONV,), 0.02),
        "w_conv_out": nrm(ks[7], (C_CONV, D_MODEL), C_CONV ** -0.5),
        "q_norm_g": 1.0 + nrm(ks[8], (HEAD_DIM,), 0.02),
        "k_norm_g": 1.0 + nrm(ks[9], (HEAD_DIM,), 0.02),
        "sinks": nrm(ks[10], (N_Q_HEADS,), 0.5),
        "w_attn_out": nrm(ks[11], (Q_W, D_MODEL), Q_W ** -0.5),
        "w_merge_out": nrm(ks[12], (D_MODEL, D_MODEL), D_MODEL ** -0.5),
        "norm_ffn_g": 1.0 + nrm(ks[13], (D_MODEL,), 0.02),
        "w_ffn_in": nrm(ks[14], (D_MODEL, 2 * D_FF), D_MODEL ** -0.5),
        "w_ffn_down": nrm(ks[15], (D_FF, D_MODEL), D_FF ** -0.5),
    }


def reference(x, norm_mix_g, w_in, conv_dw_w, conv_dw_b, conv_ln_g, conv_ln_b,
              w_conv_out, q_norm_g, k_norm_g, sinks, w_attn_out, w_merge_out,
              norm_ffn_g, w_ffn_in, w_ffn_down):
    B, S, _ = x.shape
    h = x
    for _layer in range(DEPTH):
        xn = rms_norm(h, norm_mix_g)
        u = xn @ w_in
        o1 = 2 * C_CONV
        o2 = o1 + Q_W
        o3 = o2 + KV_W
        o4 = o3 + KV_W
        o5 = o4 + D_MODEL
        conv_in = u[..., :o1]
        q = u[..., o1:o2].reshape(B, S, N_Q_HEADS, HEAD_DIM)
        k = u[..., o2:o3].reshape(B, S, N_KV_HEADS, HEAD_DIM)
        v = u[..., o3:o4].reshape(B, S, N_KV_HEADS, HEAD_DIM)
        g_conv = jax.nn.sigmoid(u[..., o4:o5])
        g_attn = jax.nn.sigmoid(u[..., o5:])

        y_conv = conv_module(conv_in, conv_dw_w, conv_dw_b, conv_ln_g, conv_ln_b, w_conv_out)
        y_attn = swa_attention(q, k, v, q_norm_g, k_norm_g, sinks) @ w_attn_out
        h = h + (g_conv * y_conv + g_attn * y_attn) @ w_merge_out

        hn = rms_norm(h, norm_ffn_g)
        gate, up = jnp.split(hn @ w_ffn_in, 2, axis=-1)
        h = h + (jax.nn.silu(gate) * up) @ w_ffn_down
    return h
```

```python
import functools
import math

import jax
import jax.numpy as jnp
from jax import lax
from jax.experimental import pallas as pl
from jax.experimental.pallas import tpu as pltpu

D_MODEL = 1024
C_CONV = 1024
CONV_WIDTH = 31
HEAD_DIM = 64
N_Q_HEADS = 16
N_KV_HEADS = 2
GROUP = N_Q_HEADS // N_KV_HEADS
BLOCK = 128
D_FF = 2816
EPS = 1e-6
NEG = -1e30

Q_W = N_Q_HEADS * HEAD_DIM
KV_W = N_KV_HEADS * HEAD_DIM
O_A = 0
O_GATE = C_CONV
O_Q = 2 * C_CONV
O_K = O_Q + Q_W
O_V = O_K + KV_W
O_GC = O_V + KV_W
O_GA = O_GC + D_MODEL

LANES = 128
TS = 512
RC = 256
CT = 256
HALO = 32
NR = 32
CR = 64
TF = 512
FT = 256

F32 = jnp.float32
BF16 = jnp.bfloat16


def _sigmoid(v):
    return 1.0 / (1.0 + jnp.exp(-v))


def _dot(a, b):
    return jnp.dot(a, b, preferred_element_type=F32)


def _dot_nt(a, b):
    return lax.dot_general(a, b, (((1,), (1,)), ((), ())), preferred_element_type=F32)


def _head_mean_matrix():
    r = lax.broadcasted_iota(jnp.int32, (LANES, LANES), 0) // HEAD_DIM
    c = lax.broadcasted_iota(jnp.int32, (LANES, LANES), 1) // HEAD_DIM
    return jnp.where(r == c, 1.0 / HEAD_DIM, 0.0).astype(BF16)


def _alibi_slope(h):
    return 2.0 ** (-8.0 * (h + 1) / N_Q_HEADS)


def _mixer_body(sinks_ref, x_ref, g1_ref, win_ref, dww_ref, dwb_ref, lng_ref, lnb_ref, wco_ref,
                qg_ref, kg_ref, wao_ref, wm_ref, h1_ref,
                xn_ref, hb_ref, cv_ref, ca_ref, qn_ref, kd_ref, vd_ref, o_ref, z_ref):
    s_idx = pl.program_id(1)

    @pl.when(s_idx == 0)
    def _():
        hb_ref[:, 0:HALO, :] = jnp.zeros((C_CONV // LANES, HALO, LANES), F32)
        kd_ref[:, 0:BLOCK, :] = jnp.zeros((N_KV_HEADS, BLOCK, LANES), BF16)
        vd_ref[:, 0:BLOCK, :] = jnp.zeros((N_KV_HEADS, BLOCK, LANES), BF16)

    def norm_step(i, carry):
        r = pl.ds(pl.multiple_of(i * NR, NR), NR)
        xv = x_ref[r, :]
        ms = jnp.mean(xv * xv, axis=-1, keepdims=True)
        xn_ref[r, :] = (xv * lax.rsqrt(ms + EPS) * g1_ref[...]).astype(BF16)
        return carry
    lax.fori_loop(0, TS // NR, norm_step, 0)

    lane = lax.broadcasted_iota(jnp.int32, (1, LANES), 1)
    lo_half = lane < HEAD_DIM
    hmean = _head_mean_matrix()

    for rc in range(TS // RC):
        rows = slice(rc * RC, (rc + 1) * RC)
        for ct in range(C_CONV // CT):
            a = _dot(xn_ref[rows, :], win_ref[:, O_A + ct * CT:O_A + (ct + 1) * CT])
            gt = _dot(xn_ref[rows, :], win_ref[:, O_GATE + ct * CT:O_GATE + (ct + 1) * CT])
            hv = a * _sigmoid(gt)
            for j in range(CT // LANES):
                hb_ref[ct * (CT // LANES) + j, HALO + rc * RC:HALO + (rc + 1) * RC, :] = hv[:, j * LANES:(j + 1) * LANES]
        for ct in range(Q_W // CT):
            q = _dot(xn_ref[rows, :], win_ref[:, O_Q + ct * CT:O_Q + (ct + 1) * CT])
            for j in range(CT // LANES):
                lt = ct * (CT // LANES) + j
                qh = q[:, j * LANES:(j + 1) * LANES]
                ms = _dot((qh * qh).astype(BF16), hmean)
                qs = qh * lax.rsqrt(ms + EPS) * (qg_ref[...] * (1.0 / math.sqrt(HEAD_DIM)))
                qn_ref[rows, lt * LANES:(lt + 1) * LANES] = qs.astype(BF16)
        k = _dot(xn_ref[rows, :], win_ref[:, O_K:O_K + KV_W])
        ms = _dot((k * k).astype(BF16), hmean)
        kn = k * lax.rsqrt(ms + EPS) * kg_ref[...]
        v = _dot(xn_ref[rows, :], win_ref[:, O_V:O_V + KV_W])
        for t, dst in ((kn, kd_ref), (v, vd_ref)):
            sw = pltpu.roll(t, HEAD_DIM, axis=1)
            dst[0, BLOCK + rc * RC:BLOCK + (rc + 1) * RC, :] = jnp.where(lo_half, t, sw).astype(BF16)
            dst[1, BLOCK + rc * RC:BLOCK + (rc + 1) * RC, :] = jnp.where(lo_half, sw, t).astype(BF16)

    for c in range(C_CONV // LANES):
        cl = slice(c * LANES, (c + 1) * LANES)
        taps = [dww_ref[k:k + 1, cl] for k in range(CONV_WIDTH)]
        bias = dwb_ref[:, cl]

        def conv_step(i, carry, c=c, cl=cl, taps=taps, bias=bias):
            t0 = pl.multiple_of(i * CR, CR)
            base = t0 + (HALO - (CONV_WIDTH - 1))
            acc = bias + taps[0] * hb_ref[c, pl.ds(base, CR), :]
            for k in range(1, CONV_WIDTH):
                acc = acc + taps[k] * hb_ref[c, pl.ds(base + k, CR), :]
            cv_ref[pl.ds(t0, CR), cl] = acc
            return carry
        lax.fori_loop(0, TS // CR, conv_step, 0)

    hb_ref[:, 0:HALO, :] = hb_ref[:, TS:TS + HALO, :]

    def ln_step(i, carry):
        r = pl.ds(pl.multiple_of(i * NR, NR), NR)
        cvv = cv_ref[r, :]
        mu = jnp.mean(cvv, axis=-1, keepdims=True)
        d = cvv - mu
        var = jnp.mean(d * d, axis=-1, keepdims=True)
        y = d * lax.rsqrt(var + EPS) * lng_ref[...] + lnb_ref[...]
        ca_ref[r, :] = (y * _sigmoid(y)).astype(BF16)
        return carry
    lax.fori_loop(0, TS // NR, ln_step, 0)

    qi = lax.broadcasted_iota(jnp.int32, (BLOCK, 2 * BLOCK), 0)
    sj = lax.broadcasted_iota(jnp.int32, (BLOCK, 2 * BLOCK), 1)
    dist_i = qi + BLOCK - sj
    band = (dist_i >= 0) & (dist_i < BLOCK)
    dist = dist_i.astype(F32)

    def attn_block(n, carry):
        r0 = pl.multiple_of(n * BLOCK, BLOCK)
        first_col = jnp.where(jnp.logical_and(s_idx == 0, n == 0), BLOCK, 0)
        valid = band & (sj >= first_col)
        for lt in range(Q_W // LANES):
            kvh = (2 * lt) // GROUP
            kw = kd_ref[kvh, pl.ds(r0, 2 * BLOCK), :]
            vw = vd_ref[kvh, pl.ds(r0, 2 * BLOCK), :]
            qt = qn_ref[pl.ds(r0, BLOCK), lt * LANES:(lt + 1) * LANES]
            outs = []
            for e in range(2):
                h = 2 * lt + e
                qm = jnp.where(lo_half if e == 0 else jnp.logical_not(lo_half), qt, jnp.zeros_like(qt))
                s = _dot_nt(qm, kw)
                s = jnp.where(valid, s - _alibi_slope(h) * dist, NEG)
                sink = sinks_ref[h]
                m = jnp.maximum(jnp.max(s, axis=-1, keepdims=True), sink)
                p = jnp.exp(s - m)
                den = jnp.sum(p, axis=-1, keepdims=True) + jnp.exp(sink - m)
                outs.append(_dot(p.astype(BF16), vw) * (1.0 / den))
            o_ref[pl.ds(r0, BLOCK), lt * LANES:(lt + 1) * LANES] = jnp.where(lo_half, outs[0], outs[1]).astype(BF16)
        return carry
    lax.fori_loop(0, TS // BLOCK, attn_block, 0)

    kd_ref[:, 0:BLOCK, :] = kd_ref[:, TS:TS + BLOCK, :]
    vd_ref[:, 0:BLOCK, :] = vd_ref[:, TS:TS + BLOCK, :]

    for rc in range(TS // RC):
        rows = slice(rc * RC, (rc + 1) * RC)
        for ct in range(D_MODEL // CT):
            cols = slice(ct * CT, (ct + 1) * CT)
            yc = _dot(ca_ref[rows, :], wco_ref[:, cols])
            gc = _dot(xn_ref[rows, :], win_ref[:, O_GC + ct * CT:O_GC + (ct + 1) * CT])
            ya = _dot(o_ref[rows, :], wao_ref[:, cols])
            ga = _dot(xn_ref[rows, :], win_ref[:, O_GA + ct * CT:O_GA + (ct + 1) * CT])
            z_ref[rows, cols] = (_sigmoid(gc) * yc + _sigmoid(ga) * ya).astype(BF16)
        h1_ref[rows, :] = x_ref[rows, :] + _dot(z_ref[rows, :], wm_ref[...])


def _ffn_body(h_ref, g2_ref, wfi_ref, wfd_ref, out_ref, hn_ref, a_ref):
    def norm_step(i, carry):
        r = pl.ds(pl.multiple_of(i * NR, NR), NR)
        hv = h_ref[r, :]
        ms = jnp.mean(hv * hv, axis=-1, keepdims=True)
        hn_ref[r, :] = (hv * lax.rsqrt(ms + EPS) * g2_ref[...]).astype(BF16)
        return carry
    lax.fori_loop(0, TF // NR, norm_step, 0)

    for rc in range(TF // RC):
        rows = slice(rc * RC, (rc + 1) * RC)
        for ft in range(D_FF // FT):
            g = _dot(hn_ref[rows, :], wfi_ref[:, ft * FT:(ft + 1) * FT])
            u = _dot(hn_ref[rows, :], wfi_ref[:, D_FF + ft * FT:D_FF + (ft + 1) * FT])
            a_ref[rows, ft * FT:(ft + 1) * FT] = (g * _sigmoid(g) * u).astype(BF16)
        out_ref[rows, :] = h_ref[rows, :] + _dot(a_ref[rows, :], wfd_ref[...])


def _resident(shape):
    return pl.BlockSpec(shape, lambda *_: (0,) * len(shape), pipeline_mode=pl.Buffered(1))


def _vmem_limit(nbytes):
    return int(nbytes)


def _mixer(x, sinks, g1, w_in, dww, dwb, lng, lnb, wco, qg, kg, wao, wm):
    B, S, _ = x.shape
    in_cols = w_in.shape[1]
    row = lambda n: _resident((1, n))
    in_specs = [
        pl.BlockSpec(memory_space=pltpu.SMEM),
        pl.BlockSpec((None, TS, D_MODEL), lambda b, s: (b, s, 0)),
        row(D_MODEL),
        _resident((D_MODEL, in_cols)),
        _resident((CONV_WIDTH, C_CONV)),
        row(C_CONV), row(C_CONV), row(C_CONV),
        _resident((C_CONV, D_MODEL)),
        row(LANES), row(LANES),
        _resident((Q_W, D_MODEL)),
        _resident((D_MODEL, D_MODEL)),
    ]
    scratch = [
        pltpu.VMEM((TS, D_MODEL), BF16),
        pltpu.VMEM((C_CONV // LANES, HALO + TS, LANES), F32),
        pltpu.VMEM((TS, C_CONV), F32),
        pltpu.VMEM((TS, C_CONV), BF16),
        pltpu.VMEM((TS, Q_W), BF16),
        pltpu.VMEM((N_KV_HEADS, BLOCK + TS, LANES), BF16),
        pltpu.VMEM((N_KV_HEADS, BLOCK + TS, LANES), BF16),
        pltpu.VMEM((TS, Q_W), BF16),
        pltpu.VMEM((TS, D_MODEL), BF16),
    ]
    return pl.pallas_call(
        _mixer_body,
        grid=(B, S // TS),
        in_specs=in_specs,
        out_specs=pl.BlockSpec((None, TS, D_MODEL), lambda b, s: (b, s, 0)),
        out_shape=jax.ShapeDtypeStruct((B, S, D_MODEL), F32),
        scratch_shapes=scratch,
        compiler_params=pltpu.CompilerParams(
            dimension_semantics=("arbitrary", "arbitrary"),
            vmem_limit_bytes=_vmem_limit(56 * 1024 * 1024)),
        name="token_mixer",
    )(sinks, x, g1, w_in, dww, dwb, lng, lnb, wco, qg, kg, wao, wm)


def _ffn(h, g2, wfi, wfd):
    T = h.shape[0]
    return pl.pallas_call(
        _ffn_body,
        grid=(T // TF,),
        in_specs=[
            pl.BlockSpec((TF, D_MODEL), lambda i: (i, 0)),
            _resident((1, D_MODEL)),
            _resident((D_MODEL, 2 * D_FF)),
            _resident((D_FF, D_MODEL)),
        ],
        out_specs=pl.BlockSpec((TF, D_MODEL), lambda i: (i, 0)),
        out_shape=jax.ShapeDtypeStruct((T, D_MODEL), F32),
        scratch_shapes=[
            pltpu.VMEM((TF, D_MODEL), BF16),
            pltpu.VMEM((TF, D_FF), BF16),
        ],
        compiler_params=pltpu.CompilerParams(
            dimension_semantics=("arbitrary",),
            vmem_limit_bytes=_vmem_limit(48 * 1024 * 1024)),
        name="swiglu_ffn",
    )(h, g2, wfi, wfd)


def kernel(x, norm_mix_g, w_in, conv_dw_w, conv_dw_b, conv_ln_g, conv_ln_b, w_conv_out, q_norm_g, k_norm_g,
           sinks, w_attn_out, w_merge_out, norm_ffn_g, w_ffn_in, w_ffn_down):
    B, S, D = x.shape
    assert D == D_MODEL and S % TS == 0 and (B * S) % TF == 0
    row = lambda v: v.reshape(1, -1).astype(F32)
    tile_heads = lambda g: jnp.tile(g.astype(F32), LANES // HEAD_DIM).reshape(1, LANES)
    h1 = _mixer(
        x, sinks.astype(F32), row(norm_mix_g), w_in.astype(BF16),
        conv_dw_w.reshape(CONV_WIDTH, C_CONV).astype(F32), row(conv_dw_b), row(conv_ln_g), row(conv_ln_b),
        w_conv_out.astype(BF16), tile_heads(q_norm_g), tile_heads(k_norm_g),
        w_attn_out.astype(BF16), w_merge_out.astype(BF16))
    out = _ffn(h1.reshape(B * S, D), row(norm_ffn_g), w_ffn_in.astype(BF16), w_ffn_down.astype(BF16))
    return out.reshape(B, S, D)
```

```python
import functools
import math

import jax
import jax.numpy as jnp
from jax import lax
from jax.experimental import pallas as pl
from jax.experimental.pallas import tpu as pltpu

D_MODEL = 1024
C_CONV = 1024
CONV_WIDTH = 31
HEAD_DIM = 64
N_Q_HEADS = 16
N_KV_HEADS = 2
GROUP = N_Q_HEADS // N_KV_HEADS
BLOCK = 128
D_FF = 2816
EPS = 1e-6
NEG = -1e30
LOG2E = 1.4426950408889634

Q_W = N_Q_HEADS * HEAD_DIM
KV_W = N_KV_HEADS * HEAD_DIM
O_A = 0
O_GATE = C_CONV
O_Q = 2 * C_CONV
O_K = O_Q + Q_W
O_V = O_K + KV_W
O_GC = O_V + KV_W
O_GA = O_GC + D_MODEL

LANES = 128
TS = 256
CT = 256
HALO = 32
NR = 32
CR = 32
FT = 256

F32 = jnp.float32
BF16 = jnp.bfloat16


def _sigmoid(v):
    return 1.0 / (1.0 + jnp.exp2(v * (-LOG2E)))


def _dot(a, b):
    return jnp.dot(a, b, preferred_element_type=F32)


def _dot_nt(a, b):
    return lax.dot_general(a, b, (((1,), (1,)), ((), ())), preferred_element_type=F32)


def _alibi_slope(h):
    return 2.0 ** (-8.0 * (h + 1) / N_Q_HEADS)


def _interleave(major, minor):
    done = 0
    for p, unit in enumerate(major):
        unit()
        upto = (len(minor) * (p + 1)) // len(major)
        for u in minor[done:upto]:
            u()
        done = upto


def _layer_body(sinks_ref, x_ref, g1_ref, win_ref, dww_ref, dwb_ref, lng_ref, lnb_ref, wco_ref,
                qg_ref, kg_ref, wao_ref, wm_ref, g2_ref, wfi_ref, wfd_ref, out_ref,
                xn_ref, hb_ref, cv_ref, ca_ref, qn_ref, kd_ref, vd_ref, o_ref, z_ref, h1_ref, hn_ref, a_ref,
                *, n_tiles, tiles_per_seq):
    i = pl.program_id(0)
    s_idx = lax.rem(jnp.minimum(i, n_tiles - 1), tiles_per_seq)
    h1_w = lax.rem(i, 2)
    h1_r = 1 - h1_w

    @pl.when(i == 0)
    def _():
        h1_ref[1] = jnp.zeros((TS, D_MODEL), F32)

    @pl.when(s_idx == 0)
    def _():
        hb_ref[:, 0:HALO, :] = jnp.zeros((C_CONV // LANES, HALO, LANES), F32)
        kd_ref[:, :, 0:BLOCK, :] = jnp.zeros((N_KV_HEADS, 2, BLOCK, LANES), BF16)
        vd_ref[:, 0:BLOCK, :] = jnp.zeros((N_KV_HEADS, BLOCK, LANES), BF16)

    lane = lax.broadcasted_iota(jnp.int32, (1, LANES), 1)
    lo_half = lane < HEAD_DIM

    def norm_unit(j):
        r = slice(j * NR, (j + 1) * NR)
        xv = x_ref[r, :]
        ms = jnp.mean(xv * xv, axis=-1, keepdims=True)
        xn_ref[r, :] = (xv * lax.rsqrt(ms + EPS) * g1_ref[...]).astype(BF16)

    def ffn_norm_unit(j):
        r = slice(j * NR, (j + 1) * NR)
        hv = h1_ref[h1_r, r, :]
        ms = jnp.mean(hv * hv, axis=-1, keepdims=True)
        hn_ref[r, :] = (hv * lax.rsqrt(ms + EPS) * g2_ref[...]).astype(BF16)

    def up_piece(ft):
        g = _dot(hn_ref[...], wfi_ref[:, ft * FT:(ft + 1) * FT])
        u = _dot(hn_ref[...], wfi_ref[:, D_FF + ft * FT:D_FF + (ft + 1) * FT])
        a_ref[:, ft * FT:(ft + 1) * FT] = (g * _sigmoid(g) * u).astype(BF16)

    def down_piece():
        out_ref[...] = h1_ref[h1_r] + _dot(a_ref[...], wfd_ref[...])

    def glu_piece(ct):
        a = _dot(xn_ref[...], win_ref[:, O_A + ct * CT:O_A + (ct + 1) * CT])
        gt = _dot(xn_ref[...], win_ref[:, O_GATE + ct * CT:O_GATE + (ct + 1) * CT])
        hv = a * _sigmoid(gt)
        for j in range(CT // LANES):
            hb_ref[ct * (CT // LANES) + j, HALO:HALO + TS, :] = hv[:, j * LANES:(j + 1) * LANES]

    def head_rms_scale(t):
        sq = t * t
        zero = jnp.zeros_like(sq)
        s_lo = jnp.sum(jnp.where(lo_half, sq, zero), axis=-1, keepdims=True)
        s_hi = jnp.sum(jnp.where(lo_half, zero, sq), axis=-1, keepdims=True)
        r_lo = lax.rsqrt(s_lo * (1.0 / HEAD_DIM) + EPS)
        r_hi = lax.rsqrt(s_hi * (1.0 / HEAD_DIM) + EPS)
        return jnp.where(lo_half, r_lo, r_hi)

    def q_piece(ct):
        q = _dot(xn_ref[...], win_ref[:, O_Q + ct * CT:O_Q + (ct + 1) * CT])
        for j in range(CT // LANES):
            lt = ct * (CT // LANES) + j
            qh = q[:, j * LANES:(j + 1) * LANES]
            qs = qh * head_rms_scale(qh) * (qg_ref[...] * (LOG2E / math.sqrt(HEAD_DIM)))
            qn_ref[:, lt * LANES:(lt + 1) * LANES] = qs.astype(BF16)

    def kv_piece():
        k = _dot(xn_ref[...], win_ref[:, O_K:O_K + KV_W])
        kn = k * head_rms_scale(k) * kg_ref[...]
        v = _dot(xn_ref[...], win_ref[:, O_V:O_V + KV_W])
        rows_d = slice(BLOCK, BLOCK + TS)
        sw = pltpu.roll(kn, HEAD_DIM, axis=1)
        zero = jnp.zeros_like(kn)
        kd_ref[0, 0, rows_d, :] = jnp.where(lo_half, kn, zero).astype(BF16)
        kd_ref[0, 1, rows_d, :] = jnp.where(lo_half, zero, sw).astype(BF16)
        kd_ref[1, 0, rows_d, :] = jnp.where(lo_half, sw, zero).astype(BF16)
        kd_ref[1, 1, rows_d, :] = jnp.where(lo_half, zero, kn).astype(BF16)
        sw = pltpu.roll(v, HEAD_DIM, axis=1)
        vd_ref[0, rows_d, :] = jnp.where(lo_half, v, sw).astype(BF16)
        vd_ref[1, rows_d, :] = jnp.where(lo_half, sw, v).astype(BF16)

    def conv_unit(c, t0):
        cl = slice(c * LANES, (c + 1) * LANES)
        base = t0 + (HALO - (CONV_WIDTH - 1))
        acc = dwb_ref[:, cl] + dww_ref[0:1, cl] * hb_ref[c, base:base + CR, :]
        for k in range(1, CONV_WIDTH):
            acc = acc + dww_ref[k:k + 1, cl] * hb_ref[c, base + k:base + k + CR, :]
        cv_ref[t0:t0 + CR, cl] = acc

    def halo_unit():
        hb_ref[:, 0:HALO, :] = hb_ref[:, TS:TS + HALO, :]

    def ln_unit(j):
        r = slice(j * NR, (j + 1) * NR)
        cvv = cv_ref[r, :]
        mu = jnp.mean(cvv, axis=-1, keepdims=True)
        d = cvv - mu
        var = jnp.mean(d * d, axis=-1, keepdims=True)
        y = d * lax.rsqrt(var + EPS) * lng_ref[...] + lnb_ref[...]
        ca_ref[r, :] = (y * _sigmoid(y)).astype(BF16)

    qi = lax.broadcasted_iota(jnp.int32, (BLOCK, BLOCK), 0)
    cj = lax.broadcasted_iota(jnp.int32, (BLOCK, BLOCK), 1)
    upper = cj > qi
    dist_c = jnp.where(upper, qi + BLOCK - cj, qi - cj).astype(F32)

    def attn_unit(n, lt):
        r0 = n * BLOCK
        kvh = (2 * lt) // GROUP
        qt = qn_ref[r0:r0 + BLOCK, lt * LANES:(lt + 1) * LANES]
        outs = []
        for e in range(2):
            h = 2 * lt + e
            s = _dot_nt(qt, kd_ref[kvh, e, r0:r0 + 2 * BLOCK, :])
            sc = jnp.where(upper, s[:, :BLOCK], s[:, BLOCK:]) - (LOG2E * _alibi_slope(h)) * dist_c
            if n == 0:
                sc = jnp.where(jnp.logical_and(upper, s_idx == 0), NEG, sc)
            sink = sinks_ref[h] * LOG2E
            m = jnp.maximum(jnp.max(sc, axis=-1, keepdims=True), sink)
            p = jnp.exp2(sc - m)
            den = jnp.sum(p, axis=-1, keepdims=True) + jnp.exp2(sink - m)
            zero = jnp.zeros_like(p)
            pf = jnp.concatenate([jnp.where(upper, p, zero), jnp.where(upper, zero, p)], axis=1)
            outs.append(_dot(pf.astype(BF16), vd_ref[kvh, r0:r0 + 2 * BLOCK, :]) * (1.0 / den))
        o_ref[r0:r0 + BLOCK, lt * LANES:(lt + 1) * LANES] = jnp.where(lo_half, outs[0], outs[1]).astype(BF16)

    def conv_out_piece(ct):
        cols = slice(ct * CT, (ct + 1) * CT)
        yc = _dot(ca_ref[...], wco_ref[:, cols])
        gc = _dot(xn_ref[...], win_ref[:, O_GC + ct * CT:O_GC + (ct + 1) * CT])
        cv_ref[:, cols] = _sigmoid(gc) * yc

    def attn_out_piece(ct):
        cols = slice(ct * CT, (ct + 1) * CT)
        ya = _dot(o_ref[...], wao_ref[:, cols])
        ga = _dot(xn_ref[...], win_ref[:, O_GA + ct * CT:O_GA + (ct + 1) * CT])
        z_ref[:, cols] = (cv_ref[:, cols] + _sigmoid(ga) * ya).astype(BF16)

    def merge_piece():
        h1_ref[h1_w] = x_ref[...] + _dot(z_ref[...], wm_ref[...])

    def kv_carry_unit():
        kd_ref[:, :, 0:BLOCK, :] = kd_ref[:, :, TS:TS + BLOCK, :]
        vd_ref[:, 0:BLOCK, :] = vd_ref[:, TS:TS + BLOCK, :]

    unit = functools.partial
    _interleave([unit(norm_unit, j) for j in range(TS // NR)], [unit(ffn_norm_unit, j) for j in range(TS // NR)])
    for ct in range(C_CONV // CT):
        glu_piece(ct)
    vpu_units = [unit(conv_unit, c, t0) for t0 in range(0, TS, CR) for c in range(C_CONV // LANES)]
    vpu_units.append(halo_unit)
    vpu_units += [unit(ln_unit, j) for j in range(TS // NR)]
    mxu_units = [unit(q_piece, ct) for ct in range(Q_W // CT)] + [kv_piece]
    mxu_units += [unit(up_piece, ft) for ft in range(D_FF // FT)] + [down_piece]
    _interleave(vpu_units, mxu_units)
    attn_units = [unit(attn_unit, n, lt) for n in range(TS // BLOCK) for lt in range(Q_W // LANES)]
    _interleave(attn_units, [unit(conv_out_piece, ct) for ct in range(D_MODEL // CT)])
    kv_carry_unit()
    for ct in range(D_MODEL // CT):
        attn_out_piece(ct)
    merge_piece()


def _resident(shape):
    return pl.BlockSpec(shape, lambda *_: (0,) * len(shape), pipeline_mode=pl.Buffered(1))


def _layer(x2d, seq_len, sinks, g1, w_in, dww, dwb, lng, lnb, wco, qg, kg, wao, wm, g2, wfi, wfd):
    T = x2d.shape[0]
    n_tiles = T // TS
    row = lambda n: _resident((1, n))
    tile = lambda index: pl.BlockSpec((TS, D_MODEL), lambda i: (index(i), 0))
    in_specs = [
        pl.BlockSpec(memory_space=pltpu.SMEM),
        tile(lambda i: jnp.minimum(i, n_tiles - 1)),
        row(D_MODEL),
        _resident((D_MODEL, w_in.shape[1])),
        _resident((CONV_WIDTH, C_CONV)),
        row(C_CONV), row(C_CONV), row(C_CONV),
        _resident((C_CONV, D_MODEL)),
        row(LANES), row(LANES),
        _resident((Q_W, D_MODEL)),
        _resident((D_MODEL, D_MODEL)),
        row(D_MODEL),
        _resident((D_MODEL, 2 * D_FF)),
        _resident((D_FF, D_MODEL)),
    ]
    scratch = [
        pltpu.VMEM((TS, D_MODEL), BF16),
        pltpu.VMEM((C_CONV // LANES, HALO + TS, LANES), F32),
        pltpu.VMEM((TS, C_CONV), F32),
        pltpu.VMEM((TS, C_CONV), BF16),
        pltpu.VMEM((TS, Q_W), BF16),
        pltpu.VMEM((N_KV_HEADS, 2, BLOCK + TS, LANES), BF16),
        pltpu.VMEM((N_KV_HEADS, BLOCK + TS, LANES), BF16),
        pltpu.VMEM((TS, Q_W), BF16),
        pltpu.VMEM((TS, D_MODEL), BF16),
        pltpu.VMEM((2, TS, D_MODEL), F32),
        pltpu.VMEM((TS, D_MODEL), BF16),
        pltpu.VMEM((TS, D_FF), BF16),
    ]
    return pl.pallas_call(
        functools.partial(_layer_body, n_tiles=n_tiles, tiles_per_seq=seq_len // TS),
        grid=(n_tiles + 1,),
        in_specs=in_specs,
        out_specs=tile(lambda i: jnp.maximum(i - 1, 0)),
        out_shape=jax.ShapeDtypeStruct((T, D_MODEL), F32),
        scratch_shapes=scratch,
        compiler_params=pltpu.CompilerParams(
            dimension_semantics=("arbitrary",),
            vmem_limit_bytes=60 * 1024 * 1024),
        name="decoder_layer",
    )(sinks, x2d, g1, w_in, dww, dwb, lng, lnb, wco, qg, kg, wao, wm, g2, wfi, wfd)


def kernel(x, norm_mix_g, w_in, conv_dw_w, conv_dw_b, conv_ln_g, conv_ln_b, w_conv_out, q_norm_g, k_norm_g,
           sinks, w_attn_out, w_merge_out, norm_ffn_g, w_ffn_in, w_ffn_down):
    B, S, D = x.shape
    assert D == D_MODEL and S % TS == 0
    row = lambda v: v.reshape(1, -1).astype(F32)
    tile_heads = lambda g: jnp.tile(g.astype(F32), LANES // HEAD_DIM).reshape(1, LANES)
    out = _layer(
        x.reshape(B * S, D), S, sinks.astype(F32), row(norm_mix_g), w_in.astype(BF16),
        conv_dw_w.reshape(CONV_WIDTH, C_CONV).astype(F32), row(conv_dw_b), row(conv_ln_g), row(conv_ln_b),
        w_conv_out.astype(BF16), tile_heads(q_norm_g), tile_heads(k_norm_g),
        w_attn_out.astype(BF16), w_merge_out.astype(BF16),
        row(norm_ffn_g), w_ffn_in.astype(BF16), w_ffn_down.astype(BF16))
    return out.reshape(B, S, D)
```
